```python
import numpy as np
import jax
import jax.numpy as jnp
from jax import lax

D_MODEL = 1024
BATCH = 8
SEQ = 2048
DEPTH = 4
DEC_BATCH = 32
DEC_SEQ = 1
PAST_LEN = 8192
PAGE_SIZE = 128

HEAD_DIM = 64
A_HEADS = 8
A_KV_HEADS = 2
A_GROUP = A_HEADS // A_KV_HEADS
A_WIDTH = A_HEADS * HEAD_DIM
A_KV = A_KV_HEADS * HEAD_DIM
A_BRANCHES = 3
CMP_LEN = 32
CMP_STRIDE = 16
SLC_LEN = 64
SLC_TOPN = 8
WINDOW = 512
A_QBLOCK = 64
B_GROUPS = 4
B_WIDTH = B_GROUPS * HEAD_DIM
CHUNK = 128
C_HEADS = 4
C_WIDTH = C_HEADS * HEAD_DIM
MOBA_BLOCK = 256
MOBA_TOPK = 3
C_QBLOCK = 32
N_BRANCH = 3
EPS = 1e-6
IN_SPLITS = (A_WIDTH, A_KV, A_KV, A_KV, A_KV, A_KV, A_KV, A_BRANCHES * A_HEADS, A_WIDTH,
             2 * B_WIDTH, B_WIDTH, C_WIDTH, C_WIDTH, C_WIDTH, C_WIDTH, N_BRANCH * D_MODEL)
IN_OFFSETS = tuple(int(o) for o in np.cumsum(IN_SPLITS)[:-1])
D_IN = int(sum(IN_SPLITS))

kernel_name = 'hybrid_nsa_gmlp_moba_step'


def rms_norm(x, g):
    xf = x.astype(jnp.float32)
    y = xf * lax.rsqrt(jnp.mean(xf * xf, axis=-1, keepdims=True) + EPS)
    return (y * g.astype(jnp.float32)).astype(x.dtype)


def layer_norm(x, g, b):
    xf = x.astype(jnp.float32)
    xc = xf - jnp.mean(xf, axis=-1, keepdims=True)
    y = xc * lax.rsqrt(jnp.mean(xc * xc, axis=-1, keepdims=True) + EPS)
    return (y * g.astype(jnp.float32) + b.astype(jnp.float32)).astype(x.dtype)


def masked_softmax(s, mask):
    s = jnp.where(mask, s.astype(jnp.float32), -jnp.inf)
    m = jnp.max(s, axis=-1, keepdims=True)
    m = jnp.where(jnp.isfinite(m), m, 0.0)
    e = jnp.where(mask, jnp.exp(s - m), 0.0)
    d = jnp.sum(e, axis=-1, keepdims=True)
    return e / jnp.where(d > 0, d, 1.0)


def over_query_blocks(fn, arrays, pos, block):
    t = pos.shape[0]
    blk = min(block, t)
    n = -(-t // blk)
    pad = n * blk - t

    def split(a):
        a = jnp.pad(a, [(0, 0), (0, pad)] + [(0, 0)] * (a.ndim - 2))
        a = a.reshape((a.shape[0], n, blk) + a.shape[2:])
        return jnp.moveaxis(a, 1, 0)

    pos_b = jnp.pad(pos, (0, pad), mode='edge').reshape(n, blk)
    out = lax.map(lambda xs: fn(xs[0], *xs[1]), (pos_b, tuple(split(a) for a in arrays)))
    out = jnp.moveaxis(out, 0, 1)
    out = out.reshape((out.shape[0], n * blk) + out.shape[3:])
    return out[:, :t]


def compress(rows, w, pe):
    n = (rows.shape[1] - CMP_LEN) // CMP_STRIDE + 1
    idx = np.arange(n)[:, None] * CMP_STRIDE + np.arange(CMP_LEN)[None, :]
    blocks = rows[:, idx] + pe[None, None, :, None, :]
    return jnp.einsum('bnlhd,lde->bnhe', blocks, w)


def overlap_matrix(nc, ns):
    cs = np.arange(nc) * CMP_STRIDE
    ss = np.arange(ns) * SLC_LEN
    ov = np.minimum(cs[:, None] + CMP_LEN, ss[None, :] + SLC_LEN) - np.maximum(cs[:, None], ss[None, :])
    return (np.clip(ov, 0, None) / CMP_STRIDE).astype(np.float32)


def nsa_attention(q, gates, pos, kc, vc, ks, vs, kw, vw, kw_pos0, cmp_wk, cmp_pk, cmp_wv, cmp_pv, kc_g):
    bsz, lk = ks.shape[:2]
    kcmp = rms_norm(compress(kc, cmp_wk, cmp_pk), kc_g)
    vcmp = compress(vc, cmp_wv, cmp_pv)
    nc = kcmp.shape[1]
    cmp_end = jnp.arange(nc, dtype=jnp.int32) * CMP_STRIDE + (CMP_LEN - 1)
    ns = -(-lk // SLC_LEN)
    ovl = jnp.asarray(overlap_matrix(nc, ns))
    n_sel = min(SLC_TOPN, ns)
    m_sel = n_sel * SLC_LEN

    def to_blocks(a):
        a = jnp.pad(a, ((0, 0), (0, ns * SLC_LEN - lk), (0, 0), (0, 0)))
        return a.reshape(bsz, ns, SLC_LEN, A_KV_HEADS, HEAD_DIM).transpose(0, 3, 1, 2, 4)

    ks_b, vs_b = to_blocks(ks), to_blocks(vs)
    kw_p = jnp.pad(kw, ((0, 0), (WINDOW, A_QBLOCK), (0, 0), (0, 0)))
    vw_p = jnp.pad(vw, ((0, 0), (WINDOW, A_QBLOCK), (0, 0), (0, 0)))
    bi = jnp.arange(bsz)[:, None, None, None]
    hi = jnp.arange(A_KV_HEADS)[None, None, :, None]
    blk_j = jnp.arange(ns, dtype=jnp.int32)
    scale = HEAD_DIM ** -0.5

    def block_fn(p, qb, gb):
        nq = p.shape[0]
        qg = qb.reshape(bsz, nq, A_KV_HEADS, A_GROUP, HEAD_DIM)
        s = jnp.einsum('bqkgd,bnkd->bqkgn', qg, kcmp) * scale
        pc = masked_softmax(s, (cmp_end[None, :] <= p[:, None])[None, :, None, None, :])
        o_cmp = jnp.einsum('bqkgn,bnkd->bqkgd', pc.astype(vcmp.dtype), vcmp)
        imp = jnp.einsum('bqkgn,nj->bqkj', pc, ovl)
        cur = (p // SLC_LEN)[:, None]
        visible = blk_j[None, :] <= cur
        forced = (blk_j[None, :] == 0) | (blk_j[None, :] == cur) | (blk_j[None, :] == cur - 1)
        score = jnp.where(forced[None, :, None, :], jnp.inf,
                          jnp.where(visible[None, :, None, :], imp, -jnp.inf))
        _, idx = lax.top_k(score, n_sel)
        kg = ks_b[bi, hi, idx].reshape(bsz, nq, A_KV_HEADS, m_sel, HEAD_DIM)
        vg = vs_b[bi, hi, idx].reshape(bsz, nq, A_KV_HEADS, m_sel, HEAD_DIM)
        kpos = (idx[..., None] * SLC_LEN + jnp.arange(SLC_LEN)).reshape(bsz, nq, A_KV_HEADS, 1, m_sel)
        s2 = jnp.einsum('bqkgd,bqkmd->bqkgm', qg, kg) * scale
        p2 = masked_softmax(s2, kpos <= p[None, :, None, None, None])
        o_slc = jnp.einsum('bqkgm,bqkmd->bqkgd', p2.astype(vg.dtype), vg)
        start = p[0] - kw_pos0
        kwin = lax.dynamic_slice_in_dim(kw_p, start, WINDOW + nq, axis=1)
        vwin = lax.dynamic_slice_in_dim(vw_p, start, WINDOW + nq, axis=1)
        wpos = p[0] - WINDOW + jnp.arange(WINDOW + nq, dtype=jnp.int32)
        dist = p[:, None] - wpos[None, :]
        m_win = (wpos[None, :] >= kw_pos0) & (dist >= 0) & (dist <= WINDOW)
        s3 = jnp.einsum('bqkgd,bmkd->bqkgm', qg, kwin) * scale
        p3 = masked_softmax(s3, m_win[None, :, None, None, :])
        o_win = jnp.einsum('bqkgm,bmkd->bqkgd', p3.astype(vwin.dtype), vwin)
        o = jnp.stack([o_cmp, o_slc, o_win], axis=-2).reshape(bsz, nq, A_HEADS, A_BRANCHES, HEAD_DIM)
        return jnp.einsum('bqhr,bqhrd->bqhd', gb, o)

    return over_query_blocks(block_fn, (q, gates), pos, A_QBLOCK)


def moba_attention(q, pos, k, v):
    bsz, lk = k.shape[:2]
    nb = -(-lk // MOBA_BLOCK)
    pad = nb * MOBA_BLOCK - lk
    k_ext = jnp.pad(k, ((0, 0), (0, pad + C_QBLOCK), (0, 0), (0, 0)))
    v_ext = jnp.pad(v, ((0, 0), (0, pad + C_QBLOCK), (0, 0), (0, 0)))
    k_b = k_ext[:, :nb * MOBA_BLOCK].reshape(bsz, nb, MOBA_BLOCK, C_HEADS, HEAD_DIM).transpose(0, 3, 1, 2, 4)
    v_b = v_ext[:, :nb * MOBA_BLOCK].reshape(bsz, nb, MOBA_BLOCK, C_HEADS, HEAD_DIM).transpose(0, 3, 1, 2, 4)
    k_mean = jnp.mean(k_b.astype(jnp.float32), axis=3)
    n_top = min(MOBA_TOPK, nb)
    m_sel = n_top * MOBA_BLOCK
    bi = jnp.arange(bsz)[:, None, None, None]
    hi = jnp.arange(C_HEADS)[None, None, :, None]
    blk_j = jnp.arange(nb, dtype=jnp.int32)
    scale = HEAD_DIM ** -0.5

    def block_fn(p, qb):
        nq = p.shape[0]
        own = p // MOBA_BLOCK
        past = blk_j[None, :] < own[:, None]
        gate = jnp.einsum('bqhd,bhjd->bqhj', qb.astype(jnp.float32), k_mean)
        gate = jnp.where(past[None, :, None, :], gate, -jnp.inf)
        _, idx = lax.top_k(gate, n_top)
        kg = k_b[bi, hi, idx].reshape(bsz, nq, C_HEADS, m_sel, HEAD_DIM)
        vg = v_b[bi, hi, idx].reshape(bsz, nq, C_HEADS, m_sel, HEAD_DIM)
        sel_ok = jnp.broadcast_to((idx < own[None, :, None, None])[..., None],
                                  idx.shape + (MOBA_BLOCK,)).reshape(bsz, nq, C_HEADS, m_sel)
        start = (p[0] // MOBA_BLOCK) * MOBA_BLOCK
        ko = lax.dynamic_slice_in_dim(k_ext, start, MOBA_BLOCK + nq, axis=1)
        vo = lax.dynamic_slice_in_dim(v_ext, start, MOBA_BLOCK + nq, axis=1)
        opos = start + jnp.arange(MOBA_BLOCK + nq, dtype=jnp.int32)
        own_ok = ((opos[None, :] // MOBA_BLOCK) == own[:, None]) & (opos[None, :] <= p[:, None])
        own_ok = jnp.broadcast_to(own_ok[None, :, None, :], (bsz, nq, C_HEADS, MOBA_BLOCK + nq))
        s = jnp.concatenate([jnp.einsum('bqhd,bqhmd->bqhm', qb, kg),
                             jnp.einsum('bqhd,bmhd->bqhm', qb, ko)], axis=-1) * scale
        pr = masked_softmax(s, jnp.concatenate([sel_ok, own_ok], axis=-1)).astype(vg.dtype)
        return (jnp.einsum('bqhm,bqhmd->bqhd', pr[..., :m_sel], vg)
                + jnp.einsum('bqhm,bmhd->bqhd', pr[..., m_sel:], vo))

    return over_query_blocks(block_fn, (q,), pos, C_QBLOCK)


def chunk_gmlp(uv, ln_g, ln_b, ws, bs):
    u, v = jnp.split(uv, 2, axis=-1)
    v = layer_norm(v, ln_g, ln_b)
    bsz, t, _ = v.shape
    n = -(-t // CHUNK)
    vp = jnp.pad(v, ((0, 0), (0, n * CHUNK - t), (0, 0))).reshape(bsz, n, CHUNK, B_GROUPS, HEAD_DIM)
    mixed = (jnp.einsum('gts,bcsgd->bctgd', jnp.tril(ws), vp)
             + jnp.swapaxes(bs, 0, 1)[None, None, :, :, None])
    return u * mixed.reshape(bsz, n * CHUNK, B_WIDTH)[:, :t], v


def layer(x, pos0, past, w):
    (norm_g, w_in, a_q_g, a_kc_g, a_ks_g, a_kw_g, cmp_wk, cmp_pk, cmp_wv, cmp_pv,
     b_ln_g, b_ln_b, b_ws, b_bs, c_q_g, c_k_g, w_br_a, w_br_b, w_br_c, w_out) = w
    bsz, t, _ = x.shape
    pos = pos0 + jnp.arange(t, dtype=jnp.int32)
    xn = rms_norm(x, norm_g)
    h = jnp.einsum('btd,de->bte', xn, w_in)
    (a_q, a_kc, a_vc, a_ks, a_vs, a_kw, a_vw, a_gate, a_z,
     b_uv, b_z, c_q, c_k, c_v, c_z, merge) = jnp.split(h, IN_OFFSETS, axis=-1)

    def heads(a, n):
        return a.reshape(bsz, t, n, HEAD_DIM)

    q_a = rms_norm(heads(a_q, A_HEADS), a_q_g)
    new_cmp = jnp.stack([heads(a_kc, A_KV_HEADS), heads(a_vc, A_KV_HEADS)], axis=2)
    new_slc = jnp.stack([rms_norm(heads(a_ks, A_KV_HEADS), a_ks_g), heads(a_vs, A_KV_HEADS)], axis=2)
    new_win = jnp.stack([rms_norm(heads(a_kw, A_KV_HEADS), a_kw_g), heads(a_vw, A_KV_HEADS)], axis=2)
    q_c = rms_norm(heads(c_q, C_HEADS), c_q_g)
    new_moba = jnp.stack([rms_norm(heads(c_k, C_HEADS), c_k_g), heads(c_v, C_HEADS)], axis=2)
    if past is None:
        cmp_rows, slc_rows, win_rows, moba_rows = new_cmp, new_slc, new_win, new_moba
        win_pos0, keep = pos0, min(WINDOW, t)
    else:
        past_cmp, past_slc, past_win, past_moba = past
        cmp_rows = jnp.concatenate([past_cmp, new_cmp], axis=1)
        slc_rows = jnp.concatenate([past_slc, new_slc], axis=1)
        win_rows = jnp.concatenate([past_win, new_win], axis=1)
        moba_rows = jnp.concatenate([past_moba, new_moba], axis=1)
        win_pos0, keep = pos0 - past_win.shape[1], past_win.shape[1]

    o_a = nsa_attention(q_a, jax.nn.sigmoid(a_gate).reshape(bsz, t, A_HEADS, A_BRANCHES), pos,
                        cmp_rows[:, :, 0], cmp_rows[:, :, 1], slc_rows[:, :, 0], slc_rows[:, :, 1],
                        win_rows[:, :, 0], win_rows[:, :, 1], win_pos0,
                        cmp_wk, cmp_pk, cmp_wv, cmp_pv, a_kc_g)
    o_b, v_rows = chunk_gmlp(jax.nn.gelu(b_uv), b_ln_g, b_ln_b, b_ws, b_bs)
    o_c = moba_attention(q_c, pos, moba_rows[:, :, 0], moba_rows[:, :, 1])

    br_a = jnp.einsum('btc,cd->btd', o_a.reshape(bsz, t, A_WIDTH) * jax.nn.silu(a_z), w_br_a)
    br_b = jnp.einsum('btc,cd->btd', o_b * jax.nn.silu(b_z), w_br_b)
    br_c = jnp.einsum('btc,cd->btd', o_c.reshape(bsz, t, C_WIDTH) * jax.nn.silu(c_z), w_br_c)
    g = jax.nn.sigmoid(merge).reshape(bsz, t, N_BRANCH, D_MODEL)
    mixed = g[:, :, 0] * br_a + g[:, :, 1] * br_b + g[:, :, 2] * br_c
    y = x + jnp.einsum('btd,de->bte', mixed, w_out)
    return y, (new_cmp, new_slc, win_rows[:, -keep:], new_moba, v_rows)


def setup_inputs(seed: int = 0) -> dict:
    key = jax.random.key(seed)
    ks = jax.random.split(key, 32)
    n_pages = PAST_LEN // PAGE_SIZE
    n_pool = (DEC_BATCH * n_pages * 5) // 4
    w_buf = min(WINDOW, PAST_LEN)

    def nrm(k, shape, scale=1.0):
        return scale * jax.random.normal(k, shape, jnp.float32)

    def gain(k, shape):
        return 1.0 + nrm(k, shape, 0.02)

    page_table = jax.random.permutation(ks[6], n_pool)[:DEC_BATCH * n_pages]
    page_table = page_table.reshape(DEC_BATCH, n_pages).astype(jnp.int32)
    return {
        'x_prompt': nrm(ks[0], (BATCH, SEQ, D_MODEL)),
        'x_sample': nrm(ks[1], (DEC_BATCH, DEC_SEQ, D_MODEL)),
        'cache_nsa_cmp_kv': nrm(ks[2], (DEPTH, n_pool, PAGE_SIZE, 2, A_KV_HEADS, HEAD_DIM)),
        'cache_nsa_slc_kv': nrm(ks[3], (DEPTH, n_pool, PAGE_SIZE, 2, A_KV_HEADS, HEAD_DIM)),
        'state_nsa_win_kv': nrm(ks[4], (DEPTH, DEC_BATCH, w_buf, 2, A_KV_HEADS, HEAD_DIM)),
        'cache_moba_kv': nrm(ks[5], (DEPTH, n_pool, PAGE_SIZE, 2, C_HEADS, HEAD_DIM)),
        'page_table': page_table,
        'norm_g': gain(ks[7], (DEPTH, D_MODEL)),
        'w_in': nrm(ks[8], (DEPTH, D_MODEL, D_IN), D_MODEL ** -0.5),
        'a_q_g': gain(ks[9], (DEPTH, HEAD_DIM)),
        'a_kc_g': gain(ks[10], (DEPTH, HEAD_DIM)),
        'a_ks_g': gain(ks[11], (DEPTH, HEAD_DIM)),
        'a_kw_g': gain(ks[12], (DEPTH, HEAD_DIM)),
        'cmp_wk': nrm(ks[13], (DEPTH, CMP_LEN, HEAD_DIM, HEAD_DIM), (CMP_LEN * HEAD_DIM) ** -0.5),
        'cmp_pk': nrm(ks[14], (DEPTH, CMP_LEN, HEAD_DIM), 0.1),
        'cmp_wv': nrm(ks[15], (DEPTH, CMP_LEN, HEAD_DIM, HEAD_DIM), (CMP_LEN * HEAD_DIM) ** -0.5),
        'cmp_pv': nrm(ks[16], (DEPTH, CMP_LEN, HEAD_DIM), 0.1),
        'b_ln_g': gain(ks[17], (DEPTH, B_WIDTH)),
        'b_ln_b': nrm(ks[18], (DEPTH, B_WIDTH), 0.02),
        'b_ws': nrm(ks[19], (DEPTH, B_GROUPS, CHUNK, CHUNK), CHUNK ** -0.5),
        'b_bs': gain(ks[20], (DEPTH, B_GROUPS, CHUNK)),
        'c_q_g': gain(ks[21], (DEPTH, HEAD_DIM)),
        'c_k_g': gain(ks[22], (DEPTH, HEAD_DIM)),
        'w_br_a': nrm(ks[23], (DEPTH, A_WIDTH, D_MODEL), A_WIDTH ** -0.5),
        'w_br_b': nrm(ks[24], (DEPTH, B_WIDTH, D_MODEL), B_WIDTH ** -0.5),
        'w_br_c': nrm(ks[25], (DEPTH, C_WIDTH, D_MODEL), C_WIDTH ** -0.5),
        'w_out': nrm(ks[26], (DEPTH, D_MODEL, D_MODEL), 0.5 * D_MODEL ** -0.5),
    }


def reference(x_prompt, x_sample, cache_nsa_cmp_kv, cache_nsa_slc_kv, state_nsa_win_kv, cache_moba_kv,
              page_table, norm_g, w_in, a_q_g, a_kc_g, a_ks_g, a_kw_g, cmp_wk, cmp_pk, cmp_wv, cmp_pv,
              b_ln_g, b_ln_b, b_ws, b_bs, c_q_g, c_k_g, w_br_a, w_br_b, w_br_c, w_out):
    past_len = page_table.shape[1] * PAGE_SIZE

    def paged(cache):
        rows = cache[page_table]
        return rows.reshape((rows.shape[0], past_len) + rows.shape[3:])

    y_prompt, y_sample = x_prompt, x_sample
    st_p, st_s = [], []
    for l in range(DEPTH):
        w = (norm_g[l], w_in[l], a_q_g[l], a_kc_g[l], a_ks_g[l], a_kw_g[l], cmp_wk[l], cmp_pk[l],
             cmp_wv[l], cmp_pv[l], b_ln_g[l], b_ln_b[l], b_ws[l], b_bs[l], c_q_g[l], c_k_g[l],
             w_br_a[l], w_br_b[l], w_br_c[l], w_out[l])
        y_prompt, sp = layer(y_prompt, 0, None, w)
        past = (paged(cache_nsa_cmp_kv[l]), paged(cache_nsa_slc_kv[l]), state_nsa_win_kv[l],
                paged(cache_moba_kv[l]))
        y_sample, ss = layer(y_sample, past_len, past, w)
        st_p.append(sp)
        st_s.append(ss)
    p_cmp = jnp.stack([s[0] for s in st_p])
    p_slc = jnp.stack([s[1] for s in st_p])
    p_win = jnp.stack([s[2] for s in st_p])
    p_moba = jnp.stack([s[3] for s in st_p])
    s_cmp = jnp.stack([s[0] for s in st_s])
    s_slc = jnp.stack([s[1] for s in st_s])
    s_win = jnp.stack([s[2] for s in st_s])
    s_moba = jnp.stack([s[3] for s in st_s])
    s_gmlp_v = jnp.stack([s[4] for s in st_s])
    return (y_prompt, y_sample, p_cmp, p_slc, p_win, p_moba, s_cmp, s_slc, s_win, s_moba, s_gmlp_v)
```

```python
import functools

import numpy as np
import jax
import jax.numpy as jnp
from jax import lax
from jax.experimental import pallas as pl
from jax.experimental.pallas import tpu as pltpu

F32, BF16 = jnp.float32, jnp.bfloat16
HIGHEST = lax.Precision.HIGHEST

D_MODEL = 1024
HEAD_DIM = 64
A_HEADS, A_KV_HEADS, A_BRANCHES = 8, 2, 3
CMP_LEN, CMP_STRIDE, SLC_LEN, SLC_TOPN, WINDOW = 32, 16, 64, 8, 512
B_GROUPS, B_WIDTH, CHUNK = 4, 256, 128
C_HEADS, C_WIDTH, MOBA_BLOCK, MOBA_TOPK = 4, 256, 256, 3
PAGE_SIZE = 128
EPS = 1e-6
SCALE = HEAD_DIM ** -0.5
NEG = -1e30
HEAD_PERM = (0, 4, 1, 5, 2, 6, 3, 7)

TM = 512
TQ = 128
TK = 128
VMEM_LIMIT = 56 * 1024 * 1024


def _nn(a, b):
    return jnp.dot(a, b, preferred_element_type=F32)


def _nt(a, b):
    return lax.dot_general(a, b, (((1,), (1,)), ((), ())), preferred_element_type=F32)


def _nn_hi(a, b):
    return jnp.dot(a, b, preferred_element_type=F32, precision=HIGHEST)


def _nt_hi(a, b):
    return lax.dot_general(a, b, (((1,), (1,)), ((), ())), preferred_element_type=F32, precision=HIGHEST)


def _split(x):
    hi = x.astype(BF16)
    lo = (x - hi.astype(F32)).astype(BF16)
    return hi, lo


def _nn_x2(x, w_bf):
    hi, lo = _split(x)
    return _nn(hi, w_bf) + _nn(lo, w_bf)


def _nt_x3(a, b):
    ah, al = _split(a)
    bh, bl = _split(b)
    return _nt(ah, bh) + _nt(ah, bl) + _nt(al, bh)


def _nn_x3(a, b):
    ah, al = _split(a)
    bh, bl = _split(b)
    return _nn(ah, bh) + _nn(ah, bl) + _nn(al, bh)


def _rms_rows(x, g_row):
    return x * lax.rsqrt(jnp.mean(x * x, axis=-1, keepdims=True) + EPS) * g_row


def _norm_heads_rows(x, bd_bf, g_row):
    ms = _nn_x2(x * x, bd_bf) * (1.0 / HEAD_DIM)
    return x * lax.rsqrt(ms + EPS) * g_row


def _norm_heads_cols(xT, g_col, n_heads):
    outs = []
    for h in range(n_heads):
        blk = xT[h * HEAD_DIM:(h + 1) * HEAD_DIM]
        ms = jnp.mean(blk * blk, axis=0, keepdims=True)
        outs.append(blk * lax.rsqrt(ms + EPS) * g_col[h * HEAD_DIM:(h + 1) * HEAD_DIM])
    return jnp.concatenate(outs, axis=0)


def _norm_heads_small(x, g_row, n_heads):
    lane = lax.broadcasted_iota(jnp.int32, x.shape, 1)
    rs = jnp.zeros_like(x)
    for h in range(n_heads):
        m = (lane // HEAD_DIM) == h
        ms = jnp.sum(jnp.where(m, x * x, 0.0), axis=-1, keepdims=True) * (1.0 / HEAD_DIM)
        rs = jnp.where(m, lax.rsqrt(ms + EPS), rs)
    return x * rs * g_row


def _layer_norm_rows(v, g_row, b_row):
    vc = v - jnp.mean(v, axis=-1, keepdims=True)
    return vc * lax.rsqrt(jnp.mean(vc * vc, axis=-1, keepdims=True) + EPS) * g_row + b_row


def _softmax_rows(s, mask):
    s = jnp.where(mask, s, NEG)
    m = jnp.max(s, axis=-1, keepdims=True)
    e = jnp.where(mask, jnp.exp(s - m), 0.0)
    d = jnp.sum(e, axis=-1, keepdims=True)
    return e / jnp.where(d > 0, d, 1.0)


def _topk_mask(score, lane, k):
    sel = jnp.zeros(score.shape, jnp.bool_)
    picks = []
    for _ in range(k):
        m = jnp.max(score, axis=-1, keepdims=True)
        first = jnp.min(jnp.where(score == m, lane, jnp.int32(1 << 20)), axis=-1, keepdims=True)
        pick = lane == first
        sel = sel | pick
        score = jnp.where(pick, NEG, score)
        picks.append(first)
    return sel, picks


def _flash(q_bf, k_of, v_of, mask_of, c_lo, c_hi, n_rows, d_v):
    def body(c, carry):
        m, l, acc = carry
        msk = mask_of(c)
        s = jnp.where(msk, _nn(q_bf, k_of(c).astype(BF16)), NEG)
        m_new = jnp.maximum(m, jnp.max(s, axis=-1, keepdims=True))
        alpha = jnp.exp(m - m_new)
        e = jnp.where(msk, jnp.exp(s - m_new), 0.0)
        l = alpha * l + jnp.sum(e, axis=-1, keepdims=True)
        acc = alpha * acc + _nt(e.astype(BF16), v_of(c).astype(BF16))
        return m_new, l, acc

    init = (jnp.full((n_rows, 1), NEG, F32), jnp.zeros((n_rows, 1), F32), jnp.zeros((n_rows, d_v), F32))
    _, l, acc = lax.fori_loop(c_lo, c_hi, body, init)
    return acc / jnp.where(l > 0, l, 1.0)


def _gmlp_mix(u, v, ws_ref, bias):
    r = lax.broadcasted_iota(jnp.int32, (CHUNK, CHUNK), 0)
    c = lax.broadcasted_iota(jnp.int32, (CHUNK, CHUNK), 1)
    wcat = jnp.concatenate([jnp.where(c <= r, ws_ref[g], 0.0) for g in range(B_GROUPS)], axis=1).astype(BF16)
    lane = lax.broadcasted_iota(jnp.int32, (CHUNK, B_WIDTH), 1)
    outs = []
    for ch in range(u.shape[0] // CHUNK):
        vc = v[ch * CHUNK:(ch + 1) * CHUNK]
        vstack = jnp.concatenate([jnp.where((lane // HEAD_DIM) == g, vc, 0.0) for g in range(B_GROUPS)], axis=0)
        mixed = _nn(wcat, vstack.astype(BF16)) + bias
        outs.append(u[ch * CHUNK:(ch + 1) * CHUNK] * mixed)
    return jnp.concatenate(outs, axis=0)


def _proj_p_kernel(x_ref, ng_ref, wq_ref, wkv_ref, wg_ref, wbuv_ref, wcq_ref, wckv_ref, wcmp_ref,
                   gq_ref, gks_ref, gkw_ref, gck_ref, gcq_ref, lng_ref, lnb_ref, ws_ref, bs_ref, bd512_ref, bd256_ref,
                   qa_ref, gate_ref, cmpT_ref, slcT_ref, winT_ref, mobaT_ref, p12_ref, qc_ref, ob_ref, rows_scr):
    xn = _rms_rows(x_ref[...], ng_ref[...]).astype(BF16)
    q = _nt(xn, wq_ref[...])
    qa_ref[...] = _norm_heads_rows(q, bd512_ref[...], gq_ref[...]) * SCALE
    gate_ref[...] = jax.nn.sigmoid(_nt(xn, wg_ref[...]))
    kvT = _nt(wkv_ref[...], xn)
    cmpT_ref[...] = kvT[0:256]
    slcT_ref[0:128, :] = _norm_heads_cols(kvT[256:384], gks_ref[...], A_KV_HEADS)
    slcT_ref[128:256, :] = kvT[384:512]
    winT_ref[0:128, :] = _norm_heads_cols(kvT[512:640], gkw_ref[...], A_KV_HEADS)
    winT_ref[128:256, :] = kvT[640:768]
    rows = _nt(xn, wkv_ref[0:256, :])
    rows_scr[0] = rows[:, 0:128]
    rows_scr[1] = rows[:, 128:256]
    n_ch = rows_scr.shape[1] // CMP_STRIDE
    acc = jnp.zeros((n_ch, 512), F32)
    for l in range(CMP_STRIDE):
        xl = jnp.concatenate([rows_scr[0, pl.ds(l, n_ch, stride=CMP_STRIDE), :],
                              rows_scr[1, pl.ds(l, n_ch, stride=CMP_STRIDE), :]], axis=1)
        acc = acc + _nn(xl.astype(BF16), wcmp_ref[l])
    p12_ref[...] = acc
    ckvT = _nt(wckv_ref[...], xn)
    mobaT_ref[0:256, :] = _norm_heads_cols(ckvT[0:256], gck_ref[...], C_HEADS)
    mobaT_ref[256:512, :] = ckvT[256:512]
    qc = _nt(xn, wcq_ref[...])
    qc_ref[...] = _norm_heads_rows(qc, bd256_ref[...], gcq_ref[...]) * SCALE
    uv = jax.nn.gelu(_nt(xn, wbuv_ref[...]))
    v = _layer_norm_rows(uv[:, B_WIDTH:], lng_ref[...], lnb_ref[...])
    ob_ref[...] = _gmlp_mix(uv[:, :B_WIDTH], v, ws_ref, bs_ref[...])


def _full(shape):
    n = len(shape)
    return pl.BlockSpec(shape, lambda *_: (0,) * n)


def _layer(shape, l):
    n = len(shape)
    return pl.BlockSpec((None,) + tuple(shape), lambda *_: (l,) + (0,) * n)


def _proj_p(l, x2, w, n_batch, seq):
    n = x2.shape[0]
    tps = seq // TM
    row = lambda width: pl.BlockSpec((TM, width), lambda i: (i, 0))
    colT = lambda rows: pl.BlockSpec((None, rows, TM), lambda i: (i // tps, 0, i % tps))
    in_specs = [row(D_MODEL), _layer((1, D_MODEL), l), _layer((512, D_MODEL), l), _layer((768, D_MODEL), l),
                _layer((128, D_MODEL), l), _layer((512, D_MODEL), l), _layer((256, D_MODEL), l),
                _layer((512, D_MODEL), l), _layer((16, 256, 512), l),
                _layer((1, 512), l), _layer((128, 1), l), _layer((128, 1), l), _layer((256, 1), l),
                _layer((1, 256), l), _layer((1, 256), l), _layer((1, 256), l), _layer((4, CHUNK, CHUNK), l),
                _layer((CHUNK, B_WIDTH), l), _full((512, 512)), _full((256, 256))]
    out_shape = [jax.ShapeDtypeStruct((n, 512), F32), jax.ShapeDtypeStruct((n, 128), F32),
                 jax.ShapeDtypeStruct((n_batch, 256, seq), F32), jax.ShapeDtypeStruct((n_batch, 256, seq), F32),
                 jax.ShapeDtypeStruct((n_batch, 256, seq), F32), jax.ShapeDtypeStruct((n_batch, 512, seq), F32),
                 jax.ShapeDtypeStruct((n // CMP_STRIDE, 512), F32), jax.ShapeDtypeStruct((n, 256), F32),
                 jax.ShapeDtypeStruct((n, 256), F32)]
    out_specs = [row(512), row(128), colT(256), colT(256), colT(256), colT(512),
                 pl.BlockSpec((TM // CMP_STRIDE, 512), lambda i: (i, 0)), row(256), row(256)]
    return pl.pallas_call(
        _proj_p_kernel, grid=(n // TM,), in_specs=in_specs, out_specs=out_specs, out_shape=out_shape,
        scratch_shapes=[pltpu.VMEM((2, TM, 128), F32)], name="proj_p",
        compiler_params=pltpu.CompilerParams(dimension_semantics=("arbitrary",), vmem_limit_bytes=VMEM_LIMIT),
    )(x2, w["norm_g"], w["wq_p"], w["wkv"], w["wg"], w["wbuv"], w["wcq"], w["wckv"], w["wcmp"],
      w["gq512"], w["gks_col"], w["gkw_col"], w["gck_col"], w["gcq256"], w["ln_g"], w["ln_b"], w["b_ws"],
      w["bs_exp"], w["bd512"], w["bd256"])


def _cmp_finish(p1, p2_next, pk_ref, wk2_ref, pv_ref, wv2_ref, kcg_ref, bd128_ref, n_valid):
    n_rows = p1.shape[0]
    bk = _nn_hi(jnp.broadcast_to(pk_ref[...], (8, pk_ref.shape[1])), wk2_ref[...])[0:1]
    bv = _nn_hi(jnp.broadcast_to(pv_ref[...], (8, pv_ref.shape[1])), wv2_ref[...])[0:1]
    ok = lax.broadcasted_iota(jnp.int32, (n_rows, 128), 0) < n_valid
    kc = jnp.where(ok, p1[:, 0:128] + p2_next[:, 0:128] + bk, 0.0)
    vc = jnp.where(ok, p1[:, 128:256] + p2_next[:, 128:256] + bv, 0.0)
    return _norm_heads_rows(kc, bd128_ref[...], kcg_ref[...]), vc


def _nsa_p_kernel(q_ref, gate_ref, p12_ref, slcT_ref, winT_ref, pk_ref, wk2_ref, pv_ref, wv2_ref, kcg_ref,
                  ovl_ref, gexp_ref, bd128_ref, o_ref, kcmp_scr, vcmp_scr):
    qi = pl.program_id(1)
    n_cmp_pad = kcmp_scr.shape[0]
    n_cmp = n_cmp_pad - 1

    @pl.when(qi == 0)
    def _():
        p12 = p12_ref[...]
        p2_next = pltpu.roll(p12[:, 256:512], n_cmp_pad - 1, axis=0)
        kc, vc = _cmp_finish(p12[:, 0:256], p2_next, pk_ref, wk2_ref, pv_ref, wv2_ref, kcg_ref, bd128_ref, n_cmp)
        kcmp_scr[...] = kc
        vcmp_scr[...] = vc

    p0 = qi * TQ
    q = q_ref[...]
    lane128 = lax.broadcasted_iota(jnp.int32, (TQ, 128), 1)
    pos = p0 + lax.broadcasted_iota(jnp.int32, (TQ, 1), 0)
    pos4 = jnp.concatenate([pos] * 4, axis=0)
    gate = gate_ref[...]
    gexp = [_nn_x2(gate, gexp_ref[:, r * 512:(r + 1) * 512]) for r in range(A_BRANCHES)]
    kcmp = kcmp_scr[...]
    vcmp_bf = vcmp_scr[...].astype(BF16)
    cmp_end = lax.broadcasted_iota(jnp.int32, (4 * TQ, n_cmp_pad), 1) * CMP_STRIDE + (CMP_LEN - 1)
    out_blocks = [jnp.zeros((TQ, 128), F32) for _ in range(4)]

    for kvh in range(A_KV_HEADS):
        head_lanes = (lane128 // HEAD_DIM) == kvh
        q4 = jnp.concatenate([jnp.where(head_lanes, q[:, 128 * j:128 * (j + 1)], 0.0) for j in range(4)], axis=0)
        q4_bf = q4.astype(BF16)
        pc = _softmax_rows(_nt_x3(q4, kcmp), cmp_end <= pos4)
        o_cmp = _nn(pc.astype(BF16), vcmp_bf)
        pc_sum = pc[0:TQ] + pc[TQ:2 * TQ] + pc[2 * TQ:3 * TQ] + pc[3 * TQ:4 * TQ]
        imp = _nn_x2(pc_sum, ovl_ref[...])
        cur = pos // SLC_LEN
        forced = (lane128 == 0) | (lane128 == cur) | (lane128 == cur - 1)
        visible = lane128 <= cur
        score = jnp.where(forced, 3e38, jnp.where(visible, imp, -1.0))
        sel, _ = _topk_mask(score, lane128, SLC_TOPN)
        sel_bf = jnp.where(sel & visible, 1.0, 0.0).astype(BF16)
        erow = lax.broadcasted_iota(jnp.int32, (128, TK), 0)
        ecol = lax.broadcasted_iota(jnp.int32, (128, TK), 1)
        kcol = lax.broadcasted_iota(jnp.int32, (4 * TQ, TK), 1)

        def lanes(c):
            return pl.ds(pl.multiple_of(c * TK, TK), TK)

        def slc_mask(c):
            expand = jnp.where(erow == c * (TK // SLC_LEN) + ecol // SLC_LEN, 1.0, 0.0).astype(BF16)
            m = _nn(sel_bf, expand) > 0.5
            m4 = jnp.concatenate([m] * 4, axis=0)
            return m4 & (c * TK + kcol <= pos4)

        def win_mask(c):
            kpos = c * TK + kcol
            return (kpos <= pos4) & (pos4 - kpos <= WINDOW)

        n_c = (p0 + TQ + TK - 1) // TK
        o_slc = _flash(q4_bf, lambda c: slcT_ref[0:128, lanes(c)], lambda c: slcT_ref[128:256, lanes(c)],
                       slc_mask, 0, n_c, 4 * TQ, 128)
        c_lo = jnp.maximum(p0 - WINDOW, 0) // TK
        o_win = _flash(q4_bf, lambda c: winT_ref[0:128, lanes(c)], lambda c: winT_ref[128:256, lanes(c)],
                       win_mask, c_lo, n_c, 4 * TQ, 128)
        for j in range(4):
            rows = slice(j * TQ, (j + 1) * TQ)
            cols = slice(128 * j, 128 * (j + 1))
            mix = gexp[0][:, cols] * o_cmp[rows] + gexp[1][:, cols] * o_slc[rows] + gexp[2][:, cols] * o_win[rows]
            out_blocks[j] = jnp.where(head_lanes, mix, out_blocks[j])

    o_ref[...] = jnp.concatenate(out_blocks, axis=1)


def _nsa_p(l, qa, gate, p12, slcT, winT, w, n_batch, seq):
    nq = seq // TQ
    nc_pad = seq // CMP_STRIDE
    row = lambda width: pl.BlockSpec((TQ, width), lambda b, i: (b * nq + i, 0))
    per_b = lambda rows: pl.BlockSpec((None, rows, seq), lambda b, i: (b, 0, 0))
    in_specs = [row(512), row(128), pl.BlockSpec((nc_pad, 512), lambda b, i: (b, 0)), per_b(256), per_b(256),
                _layer((1, 2048), l), _layer((2048, 128), l), _layer((1, 2048), l), _layer((2048, 128), l),
                _layer((1, 128), l), _full((128, 128)), _full((128, 1536)), _full((128, 128))]
    return pl.pallas_call(
        _nsa_p_kernel, grid=(n_batch, nq), in_specs=in_specs, out_specs=row(512),
        out_shape=jax.ShapeDtypeStruct((n_batch * seq, 512), F32),
        scratch_shapes=[pltpu.VMEM((nc_pad, 128), F32), pltpu.VMEM((nc_pad, 128), F32)], name="nsa_p",
        compiler_params=pltpu.CompilerParams(dimension_semantics=("arbitrary", "arbitrary"),
                                             vmem_limit_bytes=VMEM_LIMIT),
    )(qa, gate, p12, slcT, winT, w["pk_flat"], w["wk2"], w["pv_flat"], w["wv2"], w["kcg128"],
      w["ovl_p"], w["gexp"], w["bd128"])


def _moba_p_kernel(q_ref, kvT_ref, o_ref, km_scr):
    qi = pl.program_id(1)
    n_blocks = kvT_ref.shape[1] // MOBA_BLOCK
    lane_km = lax.broadcasted_iota(jnp.int32, (C_WIDTH, 128), 1)

    @pl.when(qi == 0)
    def _():
        km = jnp.zeros((C_WIDTH, 128), F32)
        for j in range(n_blocks):
            col = jnp.sum(kvT_ref[0:C_WIDTH, j * MOBA_BLOCK:(j + 1) * MOBA_BLOCK], axis=-1, keepdims=True)
            km = jnp.where(lane_km == j, col * (1.0 / MOBA_BLOCK), km)
        km_scr[...] = km

    p0 = qi * TQ
    own = p0 // MOBA_BLOCK
    q = q_ref[...]
    lane256 = lax.broadcasted_iota(jnp.int32, (TQ, C_WIDTH), 1)
    q4 = jnp.concatenate([jnp.where((lane256 // HEAD_DIM) == h, q, 0.0) for h in range(C_HEADS)], axis=0)
    pos4 = p0 + lax.broadcasted_iota(jnp.int32, (4 * TQ, 1), 0) % TQ
    lane128 = lax.broadcasted_iota(jnp.int32, (4 * TQ, 128), 1)
    past = lane128 < own
    gate = jnp.where(past, _nn_x3(q4, km_scr[...]), -1e29)
    sel, _ = _topk_mask(gate, lane128, MOBA_TOPK)
    sel_f = jnp.where(sel & past, 1.0, 0.0)
    kcol = lax.broadcasted_iota(jnp.int32, (4 * TQ, MOBA_BLOCK), 1)

    def lanes(c):
        return pl.ds(pl.multiple_of(c * MOBA_BLOCK, MOBA_BLOCK), MOBA_BLOCK)

    def mask_of(c):
        picked = jnp.max(jnp.where(lane128 == c, sel_f, 0.0), axis=-1, keepdims=True) > 0.5
        limit = jnp.where(c == own, pos4, jnp.where(picked, jnp.int32(1 << 30), jnp.int32(-1)))
        return c * MOBA_BLOCK + kcol <= limit

    o4 = _flash(q4.astype(BF16), lambda c: kvT_ref[0:C_WIDTH, lanes(c)], lambda c: kvT_ref[C_WIDTH:2 * C_WIDTH, lanes(c)],
                mask_of, 0, own + 1, 4 * TQ, C_WIDTH)
    out = jnp.zeros((TQ, C_WIDTH), F32)
    for h in range(C_HEADS):
        out = jnp.where((lane256 // HEAD_DIM) == h, o4[h * TQ:(h + 1) * TQ], out)
    o_ref[...] = out


def _moba_p(qc, mobaT, n_batch, seq):
    nq = seq // TQ
    row = pl.BlockSpec((TQ, C_WIDTH), lambda b, i: (b * nq + i, 0))
    return pl.pallas_call(
        _moba_p_kernel, grid=(n_batch, nq),
        in_specs=[row, pl.BlockSpec((None, 2 * C_WIDTH, seq), lambda b, i: (b, 0, 0))], out_specs=row,
        out_shape=jax.ShapeDtypeStruct((n_batch * seq, C_WIDTH), F32),
        scratch_shapes=[pltpu.VMEM((C_WIDTH, 128), F32)], name="moba_p",
        compiler_params=pltpu.CompilerParams(dimension_semantics=("arbitrary", "arbitrary"),
                                             vmem_limit_bytes=VMEM_LIMIT),
    )(qc, mobaT)


def _post_kernel(x_ref, ng_ref, oa_ref, ob_ref, oc_ref, waz_ref, wbz_ref, wcz_ref, wm_ref,
                 wbra_ref, wbrb_ref, wbrc_ref, wout_ref, y_ref):
    x = x_ref[...]
    xn = _rms_rows(x, ng_ref[...]).astype(BF16)
    mixed = jnp.zeros(x.shape, F32)
    for r, (o_r, wz_r, wbr_r) in enumerate(((oa_ref, waz_ref, wbra_ref), (ob_ref, wbz_ref, wbrb_ref),
                                            (oc_ref, wcz_ref, wbrc_ref))):
        z = _nt(xn, wz_r[...])
        br = _nn((o_r[...] * (z * jax.nn.sigmoid(z))).astype(BF16), wbr_r[...])
        g = jax.nn.sigmoid(_nt(xn, wm_ref[r * D_MODEL:(r + 1) * D_MODEL, :]))
        mixed = mixed + g * br
    y_ref[...] = x + _nn(mixed.astype(BF16), wout_ref[...])


def _post(l, x2, oa, ob, oc, w, waz_key, wbra_key, tm):
    n = x2.shape[0]
    row = lambda width: pl.BlockSpec((tm, width), lambda i: (i, 0))
    in_specs = [row(D_MODEL), _layer((1, D_MODEL), l), row(512), row(256), row(256),
                _layer((512, D_MODEL), l), _layer((256, D_MODEL), l), _layer((256, D_MODEL), l),
                _layer((3 * D_MODEL, D_MODEL), l), _layer((512, D_MODEL), l), _layer((256, D_MODEL), l),
                _layer((256, D_MODEL), l), _layer((D_MODEL, D_MODEL), l)]
    return pl.pallas_call(
        _post_kernel, grid=(n // tm,), in_specs=in_specs, out_specs=row(D_MODEL),
        out_shape=jax.ShapeDtypeStruct((n, D_MODEL), F32), name="post",
        compiler_params=pltpu.CompilerParams(dimension_semantics=("arbitrary",), vmem_limit_bytes=VMEM_LIMIT),
    )(x2, w["norm_g"], oa, ob, oc, w[waz_key], w["wbz"], w["wcz"], w["wmerge"], w[wbra_key], w["wbrb"],
      w["wbrc"], w["wout"])


def _proj_d_kernel(x_ref, ng_ref, wq8_ref, wkv_ref, wg_ref, wbuv_ref, wcq_ref, wckv_ref,
                   gq128_ref, gks_ref, gkw_ref, gck_ref, gcq_ref, lng_ref, lnb_ref, ws_ref, bs_ref,
                   q8_ref, gate_ref, cmp_ref, slc_ref, win_ref, moba_ref, qc_ref, ob_ref, v_ref):
    xn = _rms_rows(x_ref[...], ng_ref[...]).astype(BF16)
    q8 = _nt(xn, wq8_ref[...])
    outs = []
    for h in range(A_HEADS):
        blk = q8[:, 128 * h:128 * (h + 1)]
        ms = jnp.sum(blk * blk, axis=-1, keepdims=True) * (1.0 / HEAD_DIM)
        outs.append(blk * lax.rsqrt(ms + EPS) * gq128_ref[...] * SCALE)
    q8_ref[...] = jnp.concatenate(outs, axis=1)
    gate_ref[...] = jax.nn.sigmoid(_nt(xn, wg_ref[...]))
    kv = _nt(xn, wkv_ref[...])
    cmp_ref[...] = kv[:, 0:256]
    slc_ref[...] = jnp.concatenate([_norm_heads_small(kv[:, 256:384], gks_ref[...], A_KV_HEADS), kv[:, 384:512]], axis=1)
    win_ref[...] = jnp.concatenate([_norm_heads_small(kv[:, 512:640], gkw_ref[...], A_KV_HEADS), kv[:, 640:768]], axis=1)
    ckv = _nt(xn, wckv_ref[...])
    moba_ref[...] = jnp.concatenate([_norm_heads_small(ckv[:, 0:256], gck_ref[...], C_HEADS), ckv[:, 256:512]], axis=1)
    qc_ref[...] = _norm_heads_small(_nt(xn, wcq_ref[...]), gcq_ref[...], C_HEADS) * SCALE
    uv = jax.nn.gelu(_nt(xn, wbuv_ref[...]))
    v = _layer_norm_rows(uv[:, B_WIDTH:], lng_ref[...], lnb_ref[...])
    lane = lax.broadcasted_iota(jnp.int32, (1, B_WIDTH), 1)
    wdiag = jnp.zeros((1, B_WIDTH), F32)
    for g in range(B_GROUPS):
        wdiag = jnp.where((lane // HEAD_DIM) == g, ws_ref[g, 0:1, 0:1], wdiag)
    ob_ref[...] = uv[:, :B_WIDTH] * (v * wdiag + bs_ref[0:1, :])
    v_ref[...] = v


def _proj_d(l, x2, w):
    n = x2.shape[0]
    full2 = lambda width: pl.BlockSpec((n, width), lambda i: (0, 0))
    in_specs = [full2(D_MODEL), _layer((1, D_MODEL), l), _layer((1024, D_MODEL), l), _layer((768, D_MODEL), l),
                _layer((128, D_MODEL), l), _layer((512, D_MODEL), l), _layer((256, D_MODEL), l),
                _layer((512, D_MODEL), l),
                _layer((1, 128), l), _layer((1, 128), l), _layer((1, 128), l), _layer((1, 256), l),
                _layer((1, 256), l), _layer((1, 256), l), _layer((1, 256), l), _layer((4, CHUNK, CHUNK), l),
                _layer((CHUNK, B_WIDTH), l)]
    widths = (1024, 128, 256, 256, 256, 512, 256, 256, 256)
    return pl.pallas_call(
        _proj_d_kernel, grid=(1,), in_specs=in_specs, out_specs=[full2(wd) for wd in widths],
        out_shape=[jax.ShapeDtypeStruct((n, wd), F32) for wd in widths], name="proj_d",
        compiler_params=pltpu.CompilerParams(dimension_semantics=("arbitrary",), vmem_limit_bytes=VMEM_LIMIT),
    )(x2, w["norm_g"], w["wq8"], w["wkv"], w["wg"], w["wbuv"], w["wcq"], w["wckv"],
      w["gq128"], w["gks_row"], w["gkw_row"], w["gck_row"], w["gcq256"], w["ln_g"], w["ln_b"], w["b_ws"],
      w["bs_exp"])


def _attn_d_kernel(pt_ref, q8_ref, gate_ref, slcn_ref, winn_ref, wincol_ref, moban_ref, qc_ref,
                   xc_hbm, slc_hbm, moba_hbm, win_ref,
                   wbig_ref, pk_ref, wk2_ref, pv_ref, wv2_ref, kcg_ref, ovl_ref, bd128_ref,
                   oa_ref, oc_ref, winout_ref,
                   cbuf, mkbuf, sbuf, mvbuf, sems, *, layer, n_pages):
    s = pl.program_id(0)
    n_chunks = n_pages * (PAGE_SIZE // CMP_STRIDE)
    n_cmp = n_chunks - 1
    n_slc_past = n_pages * (PAGE_SIZE // SLC_LEN)
    n_moba_past = n_pages * PAGE_SIZE // MOBA_BLOCK
    pages_per_moba = MOBA_BLOCK // PAGE_SIZE

    def cmp_copy(n):
        return pltpu.make_async_copy(xc_hbm.at[layer, pt_ref[s, n]], cbuf.at[n], sems.at[0])

    def mk_copy(n):
        return pltpu.make_async_copy(moba_hbm.at[layer, pt_ref[s, n], 0], mkbuf.at[n], sems.at[1])

    def start_pages(n, carry):
        cmp_copy(n).start()
        mk_copy(n).start()
        return carry

    def wait_pages(n, carry):
        cmp_copy(n).wait()
        mk_copy(n).wait()
        return carry

    lax.fori_loop(0, n_pages, start_pages, 0)
    lax.fori_loop(0, n_pages, wait_pages, 0)

    q8 = q8_ref[...]
    row8 = lax.broadcasted_iota(jnp.int32, (8, 1), 0)

    p12 = jnp.zeros((n_chunks, 512), F32)
    k_step = 512
    for kc in range(CMP_STRIDE * 256 // k_step):
        xs = cbuf[:, :, kc * k_step:(kc + 1) * k_step].reshape(n_chunks, k_step)
        p12 = p12 + _nn(xs.astype(BF16), wbig_ref[kc * k_step:(kc + 1) * k_step, :])
    p2_next = pltpu.roll(p12[:, 256:512], n_chunks - 1, axis=0)
    kcmp, vcmp = _cmp_finish(p12[:, 0:256], p2_next, pk_ref, wk2_ref, pv_ref, wv2_ref, kcg_ref, bd128_ref, n_cmp)
    lane_c = lax.broadcasted_iota(jnp.int32, (8, n_chunks), 1)
    pc = _softmax_rows(_nt_hi(q8, kcmp), lane_c < n_cmp)
    o_cmp = _nn_hi(pc, vcmp)
    pc_kv = jnp.zeros(pc.shape, F32)
    for kvh in range(A_KV_HEADS):
        tot = jnp.sum(jnp.where(row8 // 4 == kvh, pc, 0.0), axis=0, keepdims=True)
        pc_kv = jnp.where(row8 == kvh, tot, pc_kv)
    imp = _nn_hi(pc_kv, ovl_ref[...])
    lane_s = lax.broadcasted_iota(jnp.int32, imp.shape, 1)
    forced = (lane_s == 0) | (lane_s == n_slc_past) | (lane_s == n_slc_past - 1)
    score = jnp.where(forced, 3e38, jnp.where(lane_s <= n_slc_past, imp, -1.0))
    _, picks = _topk_mask(score, lane_s, SLC_TOPN)

    lane_km = lax.broadcasted_iota(jnp.int32, (C_WIDTH, 128), 1)
    km = jnp.zeros((C_WIDTH, 128), F32)
    for j in range(n_moba_past):
        blk = mkbuf[pages_per_moba * j].reshape(C_WIDTH, PAGE_SIZE)
        for g in range(1, pages_per_moba):
            blk = blk + mkbuf[pages_per_moba * j + g].reshape(C_WIDTH, PAGE_SIZE)
        col = jnp.sum(blk, axis=-1, keepdims=True) * (1.0 / MOBA_BLOCK)
        km = jnp.where(lane_km == j, col, km)
    lane_q = lax.broadcasted_iota(jnp.int32, (8, C_WIDTH), 1)
    qm = jnp.where((lane_q // HEAD_DIM) == row8, jnp.broadcast_to(qc_ref[...], (8, C_WIDTH)), 0.0)
    lane_g = lax.broadcasted_iota(jnp.int32, (8, 128), 1)
    mgate = jnp.where(lane_g < n_moba_past, _nn_hi(qm, km), -1e29)
    _, mpicks = _topk_mask(mgate, lane_g, MOBA_TOPK)

    slc_idx = [[picks[r][kvh, 0] for r in range(SLC_TOPN)] for kvh in range(A_KV_HEADS)]
    moba_idx = [[mpicks[r][h, 0] for r in range(MOBA_TOPK)] for h in range(C_HEADS)]

    def slc_copy(kvh, r):
        page = pt_ref[s, jnp.minimum(slc_idx[kvh][r] // 2, n_pages - 1)]
        return pltpu.make_async_copy(slc_hbm.at[layer, page, :, kvh], sbuf.at[kvh, r], sems.at[2])

    def mv_copy(h, r, g):
        page = pt_ref[s, moba_idx[h][r] * pages_per_moba + g]
        return pltpu.make_async_copy(moba_hbm.at[layer, page, 1, h], mvbuf.at[h, r, g], sems.at[3])

    for kvh in range(A_KV_HEADS):
        for r in range(SLC_TOPN):
            slc_copy(kvh, r).start()
    for h in range(C_HEADS):
        for r in range(MOBA_TOPK):
            for g in range(pages_per_moba):
                mv_copy(h, r, g).start()
    for kvh in range(A_KV_HEADS):
        for r in range(SLC_TOPN):
            slc_copy(kvh, r).wait()
    for h in range(C_HEADS):
        for r in range(MOBA_TOPK):
            for g in range(pages_per_moba):
                mv_copy(h, r, g).wait()

    slcn = slcn_ref[...]
    winn = winn_ref[...]
    lane_p = lax.broadcasted_iota(jnp.int32, (8, PAGE_SIZE), 1)
    s_self_slc = jnp.sum(q8 * slcn[:, 0:128], axis=-1, keepdims=True)
    s_self_win = jnp.sum(q8 * winn[:, 0:128], axis=-1, keepdims=True)
    o_slc = jnp.zeros((8, HEAD_DIM), F32)
    o_win = jnp.zeros((8, HEAD_DIM), F32)
    o_cmp64 = jnp.zeros((8, HEAD_DIM), F32)
    for kvh in range(A_KV_HEADS):
        qk = q8[:, HEAD_DIM * kvh:HEAD_DIM * (kvh + 1)]
        mine = (row8 // 4) == kvh
        scores, masks = [], []
        m = s_self_slc
        for r in range(SLC_TOPN):
            j = slc_idx[kvh][r]
            ok = (lane_p // SLC_LEN == j % 2) & (j < n_slc_past)
            sc = jnp.where(ok, _nn_hi(qk, sbuf[kvh, r, 0]), NEG)
            m = jnp.maximum(m, jnp.max(sc, axis=-1, keepdims=True))
            scores.append(sc)
            masks.append(ok)
        e_self = jnp.exp(s_self_slc - m)
        den = e_self
        acc = e_self * slcn[:, 128 + HEAD_DIM * kvh:128 + HEAD_DIM * (kvh + 1)]
        for r in range(SLC_TOPN):
            e = jnp.where(masks[r], jnp.exp(scores[r] - m), 0.0)
            den = den + jnp.sum(e, axis=-1, keepdims=True)
            acc = acc + _nt_hi(e, sbuf[kvh, r, 1])
        o_slc = jnp.where(mine, acc / den, o_slc)
        sw = _nn_hi(qk, win_ref[0, kvh])
        m = jnp.maximum(s_self_win, jnp.max(sw, axis=-1, keepdims=True))
        e_self = jnp.exp(s_self_win - m)
        e = jnp.exp(sw - m)
        den = e_self + jnp.sum(e, axis=-1, keepdims=True)
        acc = e_self * winn[:, 128 + HEAD_DIM * kvh:128 + HEAD_DIM * (kvh + 1)] + _nt_hi(e, win_ref[1, kvh])
        o_win = jnp.where(mine, acc / den, o_win)
        o_cmp64 = jnp.where(mine, o_cmp[:, HEAD_DIM * kvh:HEAD_DIM * (kvh + 1)], o_cmp64)
    gate_b = jnp.broadcast_to(gate_ref[...], (8, 128))
    gates = [jnp.sum(jnp.where(lane_g == A_BRANCHES * row8 + r, gate_b, 0.0), axis=-1, keepdims=True)
             for r in range(A_BRANCHES)]
    oa_ref[...] = gates[0] * o_cmp64 + gates[1] * o_slc + gates[2] * o_win

    moban = moban_ref[...]
    s_self = jnp.sum(qm * moban[:, 0:C_WIDTH], axis=-1, keepdims=True)
    for h in range(C_HEADS):
        scores = []
        m = s_self
        for r in range(MOBA_TOPK):
            for g in range(pages_per_moba):
                page = moba_idx[h][r] * pages_per_moba + g
                sc = _nn_hi(qm, mkbuf[page].reshape(C_WIDTH, PAGE_SIZE))
                m = jnp.maximum(m, jnp.max(sc, axis=-1, keepdims=True))
                scores.append(sc)
        e_self = jnp.exp(s_self - m)
        den = e_self
        acc = e_self * moban[:, C_WIDTH + HEAD_DIM * h:C_WIDTH + HEAD_DIM * (h + 1)]
        i = 0
        for r in range(MOBA_TOPK):
            for g in range(pages_per_moba):
                e = jnp.exp(scores[i] - m)
                den = den + jnp.sum(e, axis=-1, keepdims=True)
                acc = acc + _nt_hi(e, mvbuf[h, r, g])
                i += 1
        oc_ref[h:h + 1, :] = (acc / den)[h:h + 1, :]

    old = win_ref[...].reshape(4 * HEAD_DIM, WINDOW)
    lane_w = lax.broadcasted_iota(jnp.int32, old.shape, 1)
    winout_ref[...] = jnp.where(lane_w == WINDOW - 1, wincol_ref[...], pltpu.roll(old, WINDOW - 1, axis=1)
                                ).reshape(winout_ref.shape)


def _attn_d(l, page_table, q8, gate, slcn, winn, moban, qc, xc, slcT_c, mobaT_c, winT_s, w):
    n_seq, n_pages = page_table.shape
    n_chunks = n_pages * (PAGE_SIZE // CMP_STRIDE)
    per_s = lambda *shape: pl.BlockSpec((None,) + shape, lambda s, pt: (s,) + (0,) * len(shape))
    lay = lambda *shape: pl.BlockSpec((None,) + shape, lambda s, pt: (l,) + (0,) * len(shape))
    cst = lambda *shape: pl.BlockSpec(shape, lambda s, pt: (0,) * len(shape))
    anyspec = pl.BlockSpec(memory_space=pl.ANY)
    win_spec = pl.BlockSpec((None, None, 2, A_KV_HEADS, HEAD_DIM, WINDOW), lambda s, pt: (l, s, 0, 0, 0, 0))
    in_specs = [per_s(8, 128), per_s(1, 128), per_s(1, 256), per_s(1, 256), per_s(256, 1), per_s(1, 512), per_s(1, 256),
                anyspec, anyspec, anyspec, win_spec,
                lay(CMP_STRIDE * 256, 512), lay(1, 2048), lay(2048, 128), lay(1, 2048), lay(2048, 128), lay(1, 128),
                cst(n_chunks, 256), cst(128, 128)]
    out_specs = [per_s(8, HEAD_DIM), per_s(C_HEADS, HEAD_DIM),
                 pl.BlockSpec((None, 2, A_KV_HEADS, HEAD_DIM, WINDOW), lambda s, pt: (s, 0, 0, 0, 0))]
    out_shape = [jax.ShapeDtypeStruct((n_seq, 8, HEAD_DIM), F32), jax.ShapeDtypeStruct((n_seq, C_HEADS, HEAD_DIM), F32),
                 jax.ShapeDtypeStruct((n_seq, 2, A_KV_HEADS, HEAD_DIM, WINDOW), F32)]
    scratch = [pltpu.VMEM((n_pages, PAGE_SIZE // CMP_STRIDE, CMP_STRIDE * 256), F32),
               pltpu.VMEM((n_pages, C_HEADS, HEAD_DIM, PAGE_SIZE), F32),
               pltpu.VMEM((A_KV_HEADS, SLC_TOPN, 2, HEAD_DIM, PAGE_SIZE), F32),
               pltpu.VMEM((C_HEADS, MOBA_TOPK, MOBA_BLOCK // PAGE_SIZE, HEAD_DIM, PAGE_SIZE), F32),
               pltpu.SemaphoreType.DMA((4,))]
    return pl.pallas_call(
        functools.partial(_attn_d_kernel, layer=l, n_pages=n_pages),
        grid_spec=pltpu.PrefetchScalarGridSpec(num_scalar_prefetch=1, grid=(n_seq,), in_specs=in_specs,
                                               out_specs=out_specs, scratch_shapes=scratch),
        out_shape=out_shape, name="attn_d",
        compiler_params=pltpu.CompilerParams(dimension_semantics=("arbitrary",), vmem_limit_bytes=VMEM_LIMIT),
    )(page_table, q8.reshape(n_seq, 8, 128), gate.reshape(n_seq, 1, 128), slcn.reshape(n_seq, 1, 256),
      winn.reshape(n_seq, 1, 256), winn.reshape(n_seq, 256, 1), moban.reshape(n_seq, 1, 512), qc.reshape(n_seq, 1, 256),
      xc, slcT_c, mobaT_c, winT_s,
      w["wbig"], w["pk_flat"], w["wk2"], w["pv_flat"], w["wv2"], w["kcg128"], w["ovl_d"], w["bd128"])


def _overlap(nc, ns):
    cs = np.arange(nc) * CMP_STRIDE
    ss = np.arange(ns) * SLC_LEN
    ov = np.minimum(cs[:, None] + CMP_LEN, ss[None, :] + SLC_LEN) - np.maximum(cs[:, None], ss[None, :])
    return (np.clip(ov, 0, None) / CMP_STRIDE).astype(np.float32)


def _block_diag_ones(width):
    i = np.arange(width) // HEAD_DIM
    return jnp.asarray((i[:, None] == i[None, :]).astype(np.float32), BF16)


def _prepare(seq, past_len, norm_g, w_in, a_q_g, a_kc_g, a_ks_g, a_kw_g, cmp_wk, cmp_pk, cmp_wv, cmp_pv,
             b_ln_g, b_ln_b, b_ws, b_bs, c_q_g, c_k_g, w_br_a, w_br_b, w_br_c, w_out):
    depth = w_in.shape[0]
    perm = np.asarray(HEAD_PERM)
    wt = jnp.transpose(w_in, (0, 2, 1))
    wa = wt[:, 0:1280].astype(BF16)
    wb = wt[:, 1304:].astype(BF16)
    wq = wa[:, 0:512].reshape(depth, A_HEADS, HEAD_DIM, D_MODEL)
    zeros = jnp.zeros_like(wq)
    lo = jnp.concatenate([wq, zeros], axis=2)
    hi = jnp.concatenate([zeros, wq], axis=2)
    wq8 = jnp.where((np.arange(A_HEADS) // (A_HEADS // A_KV_HEADS) == 0)[None, :, None, None], lo, hi)
    w = {
        "norm_g": norm_g[:, None, :],
        "wq_p": wq[:, perm].reshape(depth, 512, D_MODEL),
        "wq8": wq8.reshape(depth, A_HEADS * 128, D_MODEL),
        "wkv": wa[:, 512:1280],
        "wg": jnp.pad(wt[:, 1280:1304], ((0, 0), (0, 104), (0, 0))).astype(BF16),
        "waz_p": wb[:, 0:512].reshape(depth, A_HEADS, HEAD_DIM, D_MODEL)[:, perm].reshape(depth, 512, D_MODEL),
        "waz": wb[:, 0:512],
        "wbuv": wb[:, 512:1024],
        "wbz": wb[:, 1024:1280],
        "wcq": wb[:, 1280:1536],
        "wckv": wb[:, 1536:2048],
        "wcz": wb[:, 2048:2304],
        "wmerge": wb[:, 2304:5376],
        "wbra_p": w_br_a.reshape(depth, A_HEADS, HEAD_DIM, D_MODEL)[:, perm].reshape(depth, 512, D_MODEL).astype(BF16),
        "wbra": w_br_a.astype(BF16),
        "wbrb": w_br_b.astype(BF16),
        "wbrc": w_br_c.astype(BF16),
        "wout": w_out.astype(BF16),
        "gq512": jnp.tile(a_q_g, (1, A_HEADS))[:, None, :],
        "gq128": jnp.tile(a_q_g, (1, 2))[:, None, :],
        "gks_col": jnp.tile(a_ks_g, (1, A_KV_HEADS))[:, :, None],
        "gkw_col": jnp.tile(a_kw_g, (1, A_KV_HEADS))[:, :, None],
        "gck_col": jnp.tile(c_k_g, (1, C_HEADS))[:, :, None],
        "gks_row": jnp.tile(a_ks_g, (1, A_KV_HEADS))[:, None, :],
        "gkw_row": jnp.tile(a_kw_g, (1, A_KV_HEADS))[:, None, :],
        "gck_row": jnp.tile(c_k_g, (1, C_HEADS))[:, None, :],
        "gcq256": jnp.tile(c_q_g, (1, C_HEADS))[:, None, :],
        "kcg128": jnp.tile(a_kc_g, (1, A_KV_HEADS))[:, None, :],
        "ln_g": b_ln_g[:, None, :],
        "ln_b": b_ln_b[:, None, :],
        "b_ws": b_ws,
        "bs_exp": jnp.repeat(jnp.swapaxes(b_bs, 1, 2), HEAD_DIM, axis=2),
        "pk_flat": cmp_pk.reshape(depth, 1, CMP_LEN * HEAD_DIM),
        "pv_flat": cmp_pv.reshape(depth, 1, CMP_LEN * HEAD_DIM),
        "wk2": jnp.tile(cmp_wk.reshape(depth, CMP_LEN * HEAD_DIM, HEAD_DIM), (1, 1, 2)),
        "wv2": jnp.tile(cmp_wv.reshape(depth, CMP_LEN * HEAD_DIM, HEAD_DIM), (1, 1, 2)),
        "bd512": _block_diag_ones(512), "bd256": _block_diag_ones(256), "bd128": _block_diag_ones(128),
    }
    eye2 = jnp.eye(2, dtype=F32)
    bdk = jnp.einsum("ab,nlde->nladbe", eye2, cmp_wk).reshape(depth, CMP_LEN, 128, 128)
    bdv = jnp.einsum("ab,nlde->nladbe", eye2, cmp_wv).reshape(depth, CMP_LEN, 128, 128)
    z = jnp.zeros_like(bdk)
    bd4 = jnp.concatenate([jnp.concatenate([bdk, z], axis=3), jnp.concatenate([z, bdv], axis=3)], axis=2)
    wcmp = jnp.concatenate([bd4[:, :CMP_STRIDE], bd4[:, CMP_STRIDE:]], axis=3).astype(BF16)
    w["wcmp"] = wcmp
    w["wbig"] = wcmp.reshape(depth, CMP_STRIDE * 256, 512)
    nc_p = seq // CMP_STRIDE
    ovl_p = np.zeros((nc_p, 128), np.float32)
    ovl_p[:nc_p - 1, :seq // SLC_LEN] = _overlap(nc_p - 1, seq // SLC_LEN)
    w["ovl_p"] = jnp.asarray(ovl_p, BF16)
    nc_d = past_len // CMP_STRIDE
    ns_d = past_len // SLC_LEN + 1
    ovl_d = np.zeros((nc_d, 256), np.float32)
    ovl_d[:nc_d - 1, :ns_d] = _overlap(nc_d - 1, ns_d)
    w["ovl_d"] = jnp.asarray(ovl_d)
    gexp = np.zeros((128, A_BRANCHES * 512), np.float32)
    for k, h in enumerate(HEAD_PERM):
        for r in range(A_BRANCHES):
            gexp[h * A_BRANCHES + r, r * 512 + k * HEAD_DIM:r * 512 + (k + 1) * HEAD_DIM] = 1.0
    w["gexp"] = jnp.asarray(gexp, BF16)
    return w


def kernel(x_prompt, x_sample, cache_nsa_cmp_kv, cache_nsa_slc_kv, state_nsa_win_kv, cache_moba_kv, page_table, norm_g, w_in, a_q_g, a_kc_g, a_ks_g, a_kw_g, cmp_wk, cmp_pk, cmp_wv, cmp_pv, b_ln_g, b_ln_b, b_ws, b_bs, c_q_g, c_k_g, w_br_a, w_br_b, w_br_c, w_out):
    n_batch, seq, _ = x_prompt.shape
    n_seq = x_sample.shape[0]
    depth, n_pool = cache_nsa_cmp_kv.shape[:2]
    n_pages = page_table.shape[1]
    past_len = n_pages * PAGE_SIZE
    assert seq % TM == 0 and TM % CHUNK == 0 and seq % MOBA_BLOCK == 0 and MOBA_BLOCK % TQ == 0
    assert past_len % MOBA_BLOCK == 0 and x_sample.shape[1] == 1 and state_nsa_win_kv.shape[2] == WINDOW
    w = _prepare(seq, past_len, norm_g, w_in, a_q_g, a_kc_g, a_ks_g, a_kw_g, cmp_wk, cmp_pk, cmp_wv, cmp_pv,
                 b_ln_g, b_ln_b, b_ws, b_bs, c_q_g, c_k_g, w_br_a, w_br_b, w_br_c, w_out)
    xc = cache_nsa_cmp_kv.reshape(depth, n_pool, PAGE_SIZE // CMP_STRIDE, CMP_STRIDE * 256)
    slcT_c = jnp.transpose(cache_nsa_slc_kv, (0, 1, 3, 4, 5, 2))
    mobaT_c = jnp.transpose(cache_moba_kv, (0, 1, 3, 4, 5, 2))
    winT_s = jnp.transpose(state_nsa_win_kv, (0, 1, 3, 4, 5, 2))

    xp = x_prompt.reshape(n_batch * seq, D_MODEL)
    xs = x_sample.reshape(n_seq, D_MODEL)
    p_cmp, p_slc, p_win, p_moba, s_cmp, s_slc, s_win, s_moba, s_v = [], [], [], [], [], [], [], [], []
    for l in range(depth):
        qa, gate, cmpT, slcT, winT, mobaT, p12, qc, ob = _proj_p(l, xp, w, n_batch, seq)
        oa = _nsa_p(l, qa, gate, p12, slcT, winT, w, n_batch, seq)
        oc = _moba_p(qc, mobaT, n_batch, seq)
        xp = _post(l, xp, oa, ob, oc, w, "waz_p", "wbra_p", TM)
        p_cmp.append(cmpT)
        p_slc.append(slcT)
        p_win.append(winT[:, :, seq - min(WINDOW, seq):])
        p_moba.append(mobaT)

        q8, gate_s, cmpn, slcn, winn, moban, qc_s, ob_s, v_s = _proj_d(l, xs, w)
        oa_s, oc_s, win_new = _attn_d(l, page_table, q8, gate_s, slcn, winn, moban, qc_s, xc, slcT_c, mobaT_c, winT_s, w)
        xs = _post(l, xs, oa_s.reshape(n_seq, 512), ob_s, oc_s.reshape(n_seq, C_WIDTH), w, "waz", "wbra", n_seq)
        s_cmp.append(cmpn)
        s_slc.append(slcn)
        s_win.append(win_new)
        s_moba.append(moban)
        s_v.append(v_s)

    def rows_from_slabs(slabs, n_heads):
        a = jnp.stack(slabs)
        a = a.reshape(a.shape[0], a.shape[1], 2, n_heads, HEAD_DIM, a.shape[3])
        return jnp.transpose(a, (0, 1, 5, 2, 3, 4))

    return (xp.reshape(n_batch, seq, D_MODEL), xs.reshape(n_seq, 1, D_MODEL),
            rows_from_slabs(p_cmp, A_KV_HEADS), rows_from_slabs(p_slc, A_KV_HEADS),
            rows_from_slabs(p_win, A_KV_HEADS), rows_from_slabs(p_moba, C_HEADS),
            jnp.stack(s_cmp).reshape(depth, n_seq, 1, 2, A_KV_HEADS, HEAD_DIM),
            jnp.stack(s_slc).reshape(depth, n_seq, 1, 2, A_KV_HEADS, HEAD_DIM),
            jnp.transpose(jnp.stack(s_win), (0, 1, 5, 2, 3, 4)),
            jnp.stack(s_moba).reshape(depth, n_seq, 1, 2, C_HEADS, HEAD_DIM),
            jnp.stack(s_v).reshape(depth, n_seq, 1, B_WIDTH))
```
